```python
import jax, jax.numpy as jnp
from jax import lax
import numpy as np

D_MODEL = 1024
BATCH = 8
SEQ = 2048
DEPTH = 2
DEC_BATCH = 32
DEC_SEQ = 8
PAST_LEN = 16384
PAGE_SIZE = 128

MIX_HEADS = 4
HEAD_DIM = D_MODEL // 16
MIX_W = MIX_HEADS * HEAD_DIM
N_BRANCH = 4
ROPE_DIMS = HEAD_DIM // 4
ROPE_THETA = 500000.0
MOBA_BLOCK = 256
MOBA_TOPK = 3
MOBA_QBLK = 32
NSA_CMP_LEN = 32
NSA_CMP_STRIDE = 16
NSA_CMP_HID = 4 * HEAD_DIM
NSA_SEL_LEN = 64
NSA_TOPN = 16
NSA_WINDOW = 512
NSA_QBLK = 128
NSA_FORCE = 1e4
HG_CHUNK = 64
MEM_HEADS = 4
N_MEM = 256
PEER_HEADS = 8
PEER_KEYS = 128
PEER_EXPERTS = PEER_KEYS * PEER_KEYS
PEER_TOPK = 16
PEER_DQ = 256
PEER_BLOCK = 128
N_QK_GAINS = 8
IN_COLS = 9 * MIX_W + 6 * HEAD_DIM + 3 * MIX_HEADS
EPS = 1e-6
NEG = -1e30

kernel_name = 'hybrid_moba_nsa_hgrn2_peer_decode_step'


def rms_norm(x, g):
    xf = x.astype(jnp.float32)
    y = xf * lax.rsqrt(jnp.mean(xf * xf, axis=-1, keepdims=True) + EPS)
    return (y * g.astype(jnp.float32)).astype(x.dtype)


def partial_rope(x, pos):
    half = ROPE_DIMS // 2
    inv_freq = ROPE_THETA ** (-jnp.arange(half, dtype=jnp.float32) * (2.0 / ROPE_DIMS))
    ang = pos.astype(jnp.float32)[:, None] * inv_freq[None, :]
    cos = jnp.cos(ang)[None, :, None, :]
    sin = jnp.sin(ang)[None, :, None, :]
    xf = x.astype(jnp.float32)
    x1 = xf[..., :half]
    x2 = xf[..., half:ROPE_DIMS]
    out = jnp.concatenate([x1 * cos - x2 * sin, x2 * cos + x1 * sin, xf[..., ROPE_DIMS:]], axis=-1)
    return out.astype(x.dtype)


def masked_softmax(s, mask):
    mask = jnp.broadcast_to(mask, s.shape)
    p = jax.nn.softmax(jnp.where(mask, s.astype(jnp.float32), NEG), axis=-1)
    return jnp.where(mask, p, 0.0)


def blocked_map(fn, xs, block):
    n = xs[0].shape[0]
    blk = min(block, n)
    nb = -(-n // blk)
    pad = nb * blk - n
    xs = tuple(jnp.pad(a, [(0, pad)] + [(0, 0)] * (a.ndim - 1)).reshape((nb, blk) + a.shape[1:]) for a in xs)
    out = lax.map(lambda args: fn(*args), xs)
    return jax.tree_util.tree_map(lambda o: o.reshape((nb * blk,) + o.shape[2:])[:n], out)


def moba_attention(q, k, v, q_pos):
    B, L, H, dh = k.shape
    n_blk = -(-L // MOBA_BLOCK)
    pad = n_blk * MOBA_BLOCK - L
    kb = jnp.pad(k, ((0, 0), (0, pad), (0, 0), (0, 0))).reshape(B, n_blk, MOBA_BLOCK, H, dh)
    vb = jnp.pad(v, ((0, 0), (0, pad), (0, 0), (0, 0))).reshape(B, n_blk, MOBA_BLOCK, H, dh)
    k_mean = jnp.mean(kb.astype(jnp.float32), axis=2)
    n_top = min(MOBA_TOPK, n_blk)
    bi = jnp.arange(B)[:, None, None, None]
    hi = jnp.arange(H)[None, :, None, None]
    blk_ids = jnp.arange(n_blk)
    offs = jnp.arange(MOBA_BLOCK)
    scale = dh ** -0.5

    def block(qb, posb):
        qf = qb.astype(jnp.float32)
        cur = posb // MOBA_BLOCK
        gate = jnp.einsum('qbhd,bnhd->bhqn', qf, k_mean)
        past = blk_ids[None, :] < cur[:, None]
        _, top = lax.top_k(jnp.where(past, gate, NEG), n_top)
        own = jnp.broadcast_to(cur[None, None, :, None], top.shape[:3] + (1,))
        idx = jnp.concatenate([top, own], axis=-1)
        valid = jnp.concatenate([top < cur[None, None, :, None], jnp.ones(own.shape, bool)], axis=-1)
        kg = kb[bi, idx, :, hi]
        vg = vb[bi, idx, :, hi]
        s = jnp.einsum('qbhd,bhqnkd->bhqnk', qf, kg.astype(jnp.float32)) * scale
        kpos = idx[..., None] * MOBA_BLOCK + offs
        mask = valid[..., None] & (kpos <= posb[None, None, :, None, None])
        sh = s.shape
        p = masked_softmax(s.reshape(sh[:3] + (-1,)), mask.reshape(sh[:3] + (-1,)))
        return jnp.einsum('bhqm,bhqmd->qbhd', p.astype(vg.dtype), vg.reshape(sh[:3] + (-1, dh)))

    o = blocked_map(block, (q.transpose(1, 0, 2, 3), q_pos), MOBA_QBLK)
    return o.transpose(1, 0, 2, 3)


def nsa_compressed_selected(q, k_cmp, v_cmp, k_sel, v_sel, q_pos, w1, w2, pe):
    B, L, dh = k_cmp.shape
    H = q.shape[2]
    n_cmp = (L - NSA_CMP_LEN) // NSA_CMP_STRIDE + 1
    cstart = jnp.arange(n_cmp) * NSA_CMP_STRIDE
    rows = cstart[:, None] + jnp.arange(NSA_CMP_LEN)[None, :]

    def compress(a, i):
        blocks = (a[:, rows] + pe[i]).reshape(B, n_cmp, NSA_CMP_LEN * dh)
        return (jax.nn.gelu(blocks @ w1[i]) @ w2[i]).astype(a.dtype)

    kc = compress(k_cmp, 0)
    vc = compress(v_cmp, 1)
    cmp_end = cstart + NSA_CMP_LEN - 1
    n_sblk = -(-L // NSA_SEL_LEN)
    pad = n_sblk * NSA_SEL_LEN - L
    ksb = jnp.pad(k_sel, ((0, 0), (0, pad), (0, 0))).reshape(B, n_sblk, NSA_SEL_LEN, dh)
    vsb = jnp.pad(v_sel, ((0, 0), (0, pad), (0, 0))).reshape(B, n_sblk, NSA_SEL_LEN, dh)
    sstart = jnp.arange(n_sblk) * NSA_SEL_LEN
    overlap = ((cstart[:, None] <= sstart[None, :] + NSA_SEL_LEN - 1)
               & (cstart[:, None] + NSA_CMP_LEN - 1 >= sstart[None, :])).astype(jnp.float32)
    n_top = min(NSA_TOPN, n_sblk)
    bi = jnp.arange(B)[:, None, None]
    blk_ids = jnp.arange(n_sblk)
    offs = jnp.arange(NSA_SEL_LEN)
    scale = dh ** -0.5

    def block(qb, posb):
        qf = qb.astype(jnp.float32)
        s = jnp.einsum('qbhd,bnd->bhqn', qf, kc.astype(jnp.float32)) * scale
        p = masked_softmax(s, (cmp_end[None, :] <= posb[:, None])[None, None])
        o_c = jnp.einsum('bhqn,bnd->qbhd', p.astype(vc.dtype), vc)
        imp = jnp.einsum('bhqn,ns->bqs', p, overlap)
        cur = posb // NSA_SEL_LEN
        forced = (blk_ids[None] == 0) | (blk_ids[None] == cur[:, None]) | (blk_ids[None] == cur[:, None] - 1)
        score = jnp.where(blk_ids[None] > cur[:, None], NEG, imp + jnp.where(forced, NSA_FORCE, 0.0))
        _, sel = lax.top_k(score, n_top)
        kg = ksb[bi, sel]
        vg = vsb[bi, sel]
        s2 = jnp.einsum('qbhd,bqnkd->bhqnk', qf, kg.astype(jnp.float32)) * scale
        kpos = sel[..., None] * NSA_SEL_LEN + offs
        mask = ((sel <= cur[None, :, None])[..., None] & (kpos <= posb[None, :, None, None]))[:, None]
        mask = jnp.broadcast_to(mask, s2.shape)
        sh = s2.shape
        p2 = masked_softmax(s2.reshape(sh[:3] + (-1,)), mask.reshape(sh[:3] + (-1,)))
        o_s = jnp.einsum('bhqm,bqmd->qbhd', p2.astype(vg.dtype), vg.reshape(B, sh[2], -1, dh))
        return o_c, o_s

    o_c, o_s = blocked_map(block, (q.transpose(1, 0, 2, 3), q_pos), NSA_QBLK)
    return o_c.transpose(1, 0, 2, 3), o_s.transpose(1, 0, 2, 3)


def window_attention(q, k_ctx, v_ctx, n_buf):
    B, Tq, H, dh = q.shape
    blk = min(NSA_QBLK, Tq)
    nb = -(-Tq // blk)
    pad_r = nb * blk - Tq
    kp = jnp.pad(k_ctx, ((0, 0), (NSA_WINDOW, pad_r), (0, 0)))
    vp = jnp.pad(v_ctx, ((0, 0), (NSA_WINDOW, pad_r), (0, 0)))
    span = NSA_WINDOW + blk
    scale = dh ** -0.5

    def block(qb, jb):
        start = n_buf + jb[0]
        ks = lax.dynamic_slice_in_dim(kp, start, span, axis=1)
        vs = lax.dynamic_slice_in_dim(vp, start, span, axis=1)
        cq = start + jnp.arange(blk)
        ck = start - NSA_WINDOW + jnp.arange(span)
        d = cq[:, None] - ck[None, :]
        mask = (d >= 0) & (d <= NSA_WINDOW) & (ck[None, :] >= 0)
        s = jnp.einsum('qbhd,bkd->bhqk', qb.astype(jnp.float32), ks.astype(jnp.float32)) * scale
        p = masked_softmax(s, mask[None, None])
        return jnp.einsum('bhqk,bkd->qbhd', p.astype(vs.dtype), vs)

    o = blocked_map(block, (q.transpose(1, 0, 2, 3), jnp.arange(Tq, dtype=jnp.int32)), NSA_QBLK)
    return o.transpose(1, 0, 2, 3)


def hgrn2_scan(q, k, v, logf, s0):
    B, T, H, dk = q.shape
    C = min(HG_CHUNK, T)
    nc = -(-T // C)
    pad = nc * C - T

    def prep(a):
        a = jnp.pad(a, ((0, 0), (0, pad), (0, 0), (0, 0)))
        return a.reshape(B, nc, C, H, a.shape[-1]).transpose(1, 0, 3, 2, 4)

    causal = jnp.tril(jnp.ones((C, C), bool))

    def step(S, inp):
        qc, kc, vc, gc = inp
        G = jnp.cumsum(gc, axis=2)
        o = jnp.einsum('bhtd,bhde->bhte', qc * jnp.exp(G), S)
        decay = jnp.exp(jnp.where(causal[:, :, None], G[:, :, :, None, :] - G[:, :, None, :, :], -jnp.inf))
        A = jnp.einsum('bhtd,bhtsd,bhsd->bhts', qc, decay, kc)
        o = o + jnp.einsum('bhts,bhse->bhte', A, vc)
        G_last = G[:, :, -1:]
        S = jnp.exp(G_last[:, :, 0])[..., None] * S + jnp.einsum('bhsd,bhse->bhde', kc * jnp.exp(G_last - G), vc)
        return S, o

    S, o = lax.scan(step, s0, (prep(q), prep(k), prep(v), prep(logf)))
    o = o.transpose(1, 0, 3, 2, 4).reshape(B, nc * C, H, v.shape[-1])[:, :T]
    return o, S


def peer_ffn(xn, wq, sub_keys, w_u, w_v):
    B, T, D = xn.shape

    def block(xb):
        n = xb.shape[0]
        q = (xb @ wq).reshape(n, PEER_HEADS, 2, PEER_DQ // 2).astype(jnp.float32)
        s1 = jnp.einsum('thd,nd->thn', q[:, :, 0], sub_keys[0].astype(jnp.float32))
        s2 = jnp.einsum('thd,nd->thn', q[:, :, 1], sub_keys[1].astype(jnp.float32))
        v1, i1 = lax.top_k(s1, PEER_TOPK)
        v2, i2 = lax.top_k(s2, PEER_TOPK)
        cand = (v1[..., :, None] + v2[..., None, :]).reshape(n, PEER_HEADS, PEER_TOPK * PEER_TOPK)
        sc, ci = lax.top_k(cand, PEER_TOPK)
        e = (jnp.take_along_axis(i1, ci // PEER_TOPK, axis=-1) * PEER_KEYS
             + jnp.take_along_axis(i2, ci % PEER_TOPK, axis=-1))
        g = jax.nn.softmax(sc, axis=-1)
        u = w_u[e]
        v = w_v[e]
        h = jax.nn.gelu(jnp.einsum('td,thkd->thk', xb, u).astype(jnp.float32))
        return jnp.einsum('thk,thkd->td', (g * h).astype(v.dtype), v)

    out = blocked_map(block, (xn.reshape(B * T, D),), PEER_BLOCK)
    return out.reshape(B, T, D).astype(xn.dtype)


def memory_kv(mem, norm_mem, w_mem_kv, k_gain):
    Bm, M, _ = mem.shape
    kv = (rms_norm(mem, norm_mem) @ w_mem_kv).reshape(Bm, M, 2, MEM_HEADS, HEAD_DIM)
    return jnp.stack([rms_norm(kv[:, :, 0], k_gain), kv[:, :, 1]], axis=2)


def token_mixers(xn, pos, moba_past, nsa_past, win_buf, hg_state, mem_kv, lb,
                 w_in, w_gate, w_branch, w_out, qk_gain, nsa_cmp_w1, nsa_cmp_w2, nsa_cmp_pe, hgrn_norm):
    B, T, D = xn.shape
    dt = xn.dtype
    sizes = [MIX_W] * 4 + [6 * HEAD_DIM, 3 * MIX_HEADS] + [MIX_W] * 5
    mq, mk, mv, nq, nkv, ngate, hq, hf, hin, hgate, memq = jnp.split(xn @ w_in, np.cumsum(sizes)[:-1].tolist(), axis=-1)

    def heads(a):
        return a.reshape(B, T, MIX_HEADS, HEAD_DIM)

    q = partial_rope(rms_norm(heads(mq), qk_gain[0]), pos)
    k = partial_rope(rms_norm(heads(mk), qk_gain[1]), pos)
    moba_new = jnp.stack([k, heads(mv)], axis=2)
    moba_all = jnp.concatenate([moba_past, moba_new], axis=1)
    o_moba = moba_attention(q, moba_all[:, :, 0], moba_all[:, :, 1], pos)

    q = partial_rope(rms_norm(heads(nq), qk_gain[2]), pos)
    kv6 = nkv.reshape(B, T, 6, HEAD_DIM)

    def nsa_key(i, g):
        return partial_rope(rms_norm(kv6[:, :, i], qk_gain[g])[:, :, None], pos)[:, :, 0]

    k_cmp, k_sel, k_win = nsa_key(0, 3), nsa_key(2, 4), nsa_key(4, 5)
    nsa_new = jnp.stack([k_cmp, kv6[:, :, 1], k_sel, kv6[:, :, 3]], axis=2)
    nsa_all = jnp.concatenate([nsa_past, nsa_new], axis=1)
    o_cmp, o_sel = nsa_compressed_selected(q, nsa_all[:, :, 0], nsa_all[:, :, 1], nsa_all[:, :, 2],
                                           nsa_all[:, :, 3], pos, nsa_cmp_w1, nsa_cmp_w2, nsa_cmp_pe)
    win_ctx = jnp.concatenate([win_buf, jnp.stack([k_win, kv6[:, :, 5]], axis=2)], axis=1)
    o_win = window_attention(q, win_ctx[:, :, 0], win_ctx[:, :, 1], win_buf.shape[1])
    win_new = win_ctx[:, win_ctx.shape[1] - min(NSA_WINDOW, win_ctx.shape[1]):]
    g = jax.nn.sigmoid(ngate.reshape(B, T, MIX_HEADS, 3, 1).astype(jnp.float32))
    o_nsa = (g[:, :, :, 0] * o_cmp + g[:, :, :, 1] * o_sel + g[:, :, :, 2] * o_win).astype(dt)

    lbh = lb.reshape(MIX_HEADS, HEAD_DIM)
    logf = jnp.logaddexp(jnp.log(lbh), jnp.log1p(-lbh) + jax.nn.log_sigmoid(heads(hf).astype(jnp.float32)))
    o_hg, hg_new = hgrn2_scan(heads(hq).astype(jnp.float32), -jnp.expm1(logf), heads(hin).astype(jnp.float32),
                              logf, hg_state.astype(jnp.float32))
    o_hg = (rms_norm(o_hg, hgrn_norm) * jax.nn.silu(heads(hgate).astype(jnp.float32))).astype(dt)

    qm = rms_norm(memq.reshape(B, T, MEM_HEADS, HEAD_DIM), qk_gain[6])
    s = jnp.einsum('bthd,bmhd->bhtm', qm.astype(jnp.float32), mem_kv[:, :, 0].astype(jnp.float32)) * HEAD_DIM ** -0.5
    p = jax.nn.softmax(s, axis=-1)
    o_mem = jnp.einsum('bhtm,bmhd->bthd', p.astype(mem_kv.dtype), mem_kv[:, :, 1]).astype(dt)

    branches = jnp.stack([o_moba, o_nsa, o_hg, o_mem], axis=2).reshape(B, T, N_BRANCH, MIX_W)
    up = jnp.einsum('btnc,ncd->btnd', branches, w_branch)
    gates = jax.nn.sigmoid((xn @ w_gate).astype(jnp.float32)).reshape(B, T, N_BRANCH, D)
    y = jnp.sum(gates * up, axis=2).astype(dt) @ w_out
    return y.astype(dt), moba_new, nsa_new, win_new, hg_new.astype(hg_state.dtype)


def decoder_layer(x, pos, moba_past, nsa_past, win_buf, hg_state, mem_kv, lb,
                  norm_attn, norm_ffn, w_in, w_gate, w_branch, w_out, qk_gain, nsa_cmp_w1, nsa_cmp_w2,
                  nsa_cmp_pe, hgrn_norm, peer_wq, peer_keys, peer_u, peer_v):
    y, moba_new, nsa_new, win_new, hg_new = token_mixers(
        rms_norm(x, norm_attn), pos, moba_past, nsa_past, win_buf, hg_state, mem_kv, lb,
        w_in, w_gate, w_branch, w_out, qk_gain, nsa_cmp_w1, nsa_cmp_w2, nsa_cmp_pe, hgrn_norm)
    x = x + y
    x = x + peer_ffn(rms_norm(x, norm_ffn), peer_wq, peer_keys, peer_u, peer_v)
    return x, moba_new, nsa_new, win_new, hg_new


def setup_inputs(seed: int = 0) -> dict:
    key = jax.random.key(seed)
    ks = jax.random.split(key, 28)
    n_pages = PAST_LEN // PAGE_SIZE
    n_pool = (DEC_BATCH * n_pages * 5) // 4

    def normal(k, shape, scale):
        return jax.random.normal(k, shape, jnp.float32) * scale

    def gain(k, shape):
        return 1.0 + normal(k, shape, 0.1)

    page_table = jax.random.permutation(ks[7], n_pool)[:DEC_BATCH * n_pages].astype(jnp.int32).reshape(DEC_BATCH, n_pages)
    return {
        'x_prompt': normal(ks[0], (BATCH, SEQ, D_MODEL), 1.0),
        'x_sample': normal(ks[1], (DEC_BATCH, DEC_SEQ, D_MODEL), 1.0),
        'cache_moba_kv': normal(ks[2], (DEPTH, n_pool, PAGE_SIZE, 2, MIX_HEADS, HEAD_DIM), 1.0),
        'cache_nsa_kv': normal(ks[3], (DEPTH, n_pool, PAGE_SIZE, 4, HEAD_DIM), 1.0),
        'state_nsa_win': normal(ks[4], (DEPTH, DEC_BATCH, min(NSA_WINDOW, PAST_LEN), 2, HEAD_DIM), 1.0),
        'state_hgrn': normal(ks[5], (DEPTH, DEC_BATCH, MIX_HEADS, HEAD_DIM, HEAD_DIM), 0.5),
        'cache_mem_kv': normal(ks[6], (DEPTH, DEC_BATCH, N_MEM, 2, MEM_HEADS, HEAD_DIM), 1.0),
        'page_table': page_table,
        'mem_prompt': normal(ks[8], (BATCH, N_MEM, D_MODEL), 1.0),
        'norm_attn': gain(ks[9], (DEPTH, D_MODEL)),
        'norm_ffn': gain(ks[10], (DEPTH, D_MODEL)),
        'norm_mem': gain(ks[11], (DEPTH, D_MODEL)),
        'w_in': normal(ks[12], (DEPTH, D_MODEL, IN_COLS), D_MODEL ** -0.5),
        'w_gate': normal(ks[13], (DEPTH, D_MODEL, N_BRANCH * D_MODEL), D_MODEL ** -0.5),
        'w_branch': normal(ks[14], (DEPTH, N_BRANCH, MIX_W, D_MODEL), MIX_W ** -0.5),
        'w_out': normal(ks[15], (DEPTH, D_MODEL, D_MODEL), 0.5 * D_MODEL ** -0.5),
        'qk_gain': gain(ks[16], (DEPTH, N_QK_GAINS, HEAD_DIM)),
        'nsa_cmp_w1': normal(ks[17], (DEPTH, 2, NSA_CMP_LEN * HEAD_DIM, NSA_CMP_HID), (NSA_CMP_LEN * HEAD_DIM) ** -0.5),
        'nsa_cmp_w2': normal(ks[18], (DEPTH, 2, NSA_CMP_HID, HEAD_DIM), NSA_CMP_HID ** -0.5),
        'nsa_cmp_pe': normal(ks[19], (DEPTH, 2, NSA_CMP_LEN, HEAD_DIM), 0.1),
        'hgrn_lb': normal(ks[20], (DEPTH, MIX_HEADS * HEAD_DIM), 1.0),
        'hgrn_norm': gain(ks[21], (DEPTH, HEAD_DIM)),
        'w_mem_kv': normal(ks[22], (DEPTH, D_MODEL, 2 * MEM_HEADS * HEAD_DIM), D_MODEL ** -0.5),
        'peer_wq': normal(ks[23], (DEPTH, D_MODEL, PEER_HEADS * PEER_DQ), D_MODEL ** -0.5),
        'peer_keys': normal(ks[24], (DEPTH, 2, PEER_KEYS, PEER_DQ // 2), (PEER_DQ // 2) ** -0.5),
        'peer_u': normal(ks[25], (DEPTH, PEER_EXPERTS, D_MODEL), D_MODEL ** -0.5),
        'peer_v': normal(ks[26], (DEPTH, PEER_EXPERTS, D_MODEL), 0.5 * PEER_HEADS ** -0.5),
    }


def reference(x_prompt, x_sample, cache_moba_kv, cache_nsa_kv, state_nsa_win, state_hgrn, cache_mem_kv,
              page_table, mem_prompt, norm_attn, norm_ffn, norm_mem, w_in, w_gate, w_branch, w_out, qk_gain,
              nsa_cmp_w1, nsa_cmp_w2, nsa_cmp_pe, hgrn_lb, hgrn_norm, w_mem_kv, peer_wq, peer_keys, peer_u, peer_v):
    Bp, Tp, _ = x_prompt.shape
    Bs, Ts, _ = x_sample.shape
    past_len = page_table.shape[1] * PAGE_SIZE
    pos_p = jnp.arange(Tp, dtype=jnp.int32)
    pos_s = past_len + jnp.arange(Ts, dtype=jnp.int32)
    lb_cum = jnp.cumsum(jax.nn.softmax(hgrn_lb.astype(jnp.float32), axis=0), axis=0)
    lb_all = lb_cum - lb_cum[0:1]
    dt = x_prompt.dtype
    empty_moba = jnp.zeros((Bp, 0, 2, MIX_HEADS, HEAD_DIM), dt)
    empty_nsa = jnp.zeros((Bp, 0, 4, HEAD_DIM), dt)
    empty_win = jnp.zeros((Bp, 0, 2, HEAD_DIM), dt)
    zero_state = jnp.zeros((Bp, MIX_HEADS, HEAD_DIM, HEAD_DIM), dt)
    hp, hs = x_prompt, x_sample
    moba_p, moba_s, nsa_p, nsa_s, win_p, win_s, hg_p, hg_s, mem_p = [], [], [], [], [], [], [], [], []
    for l in range(DEPTH):
        lw = dict(norm_attn=norm_attn[l], norm_ffn=norm_ffn[l], w_in=w_in[l], w_gate=w_gate[l],
                  w_branch=w_branch[l], w_out=w_out[l], qk_gain=qk_gain[l], nsa_cmp_w1=nsa_cmp_w1[l],
                  nsa_cmp_w2=nsa_cmp_w2[l], nsa_cmp_pe=nsa_cmp_pe[l], hgrn_norm=hgrn_norm[l],
                  peer_wq=peer_wq[l], peer_keys=peer_keys[l], peer_u=peer_u[l], peer_v=peer_v[l])
        mkv = memory_kv(mem_prompt, norm_mem[l], w_mem_kv[l], qk_gain[l, 7])
        hp, mo, ns, wi, hg = decoder_layer(hp, pos_p, empty_moba, empty_nsa, empty_win, zero_state, mkv,
                                           lb_all[l], **lw)
        moba_p.append(mo); nsa_p.append(ns); win_p.append(wi); hg_p.append(hg); mem_p.append(mkv)
        moba_past = cache_moba_kv[l, page_table].reshape(Bs, past_len, 2, MIX_HEADS, HEAD_DIM)
        nsa_past = cache_nsa_kv[l, page_table].reshape(Bs, past_len, 4, HEAD_DIM)
        hs, mo, ns, wi, hg = decoder_layer(hs, pos_s, moba_past, nsa_past, state_nsa_win[l], state_hgrn[l],
                                           cache_mem_kv[l], lb_all[l], **lw)
        moba_s.append(mo); nsa_s.append(ns); win_s.append(wi); hg_s.append(hg)
    new_moba_kv_prompt = jnp.stack(moba_p)
    new_moba_kv_sample = jnp.stack(moba_s)
    new_nsa_kv_prompt = jnp.stack(nsa_p)
    new_nsa_kv_sample = jnp.stack(nsa_s)
    new_win_prompt = jnp.stack(win_p)
    new_win_sample = jnp.stack(win_s)
    new_hgrn_prompt = jnp.stack(hg_p)
    new_hgrn_sample = jnp.stack(hg_s)
    new_mem_kv_prompt = jnp.stack(mem_p)
    return (hp, hs, new_moba_kv_prompt, new_moba_kv_sample, new_nsa_kv_prompt, new_nsa_kv_sample,
            new_win_prompt, new_win_sample, new_hgrn_prompt, new_hgrn_sample, new_mem_kv_prompt)
```

```python
import functools

import jax
import jax.numpy as jnp
from jax import lax
from jax.experimental import pallas as pl
from jax.experimental.pallas import tpu as pltpu

f32 = jnp.float32
bf16 = jnp.bfloat16
HI = lax.Precision.HIGHEST

D = 1024
HD = 64
NH = 4
MIXW = NH * HD
EPS = 1e-6
NEG = -1e30
ROPE_HALF = 8
ROPE_THETA = 500000.0
PAGE = 128
MOBA_BLOCK = 256
MOBA_TOPK = 3
CMP_LEN = 32
CMP_STRIDE = 16
CMP_HID = 256
SEL_LEN = 64
NSA_TOPN = 16
NSA_WINDOW = 512
NSA_FORCE = 1e4
N_MEM = 256
PEER_HEADS = 8
PEER_KEYS = 128
PEER_TOPK = 16
N_EXPERTS = PEER_KEYS * PEER_KEYS
SCALE = HD ** -0.5

TQ = 256
PG = 8
LANES = 128

S_MQ, S_MK, S_MV, S_NQ = 0, 4, 8, 12
S_KCMP, S_VCMP, S_KSEL, S_VSEL, S_KWIN, S_VWIN = 16, 17, 18, 19, 20, 21
S_HQ, S_HF, S_HIN, S_HGATE, S_MEMQ, S_NGATE = 22, 26, 30, 34, 38, 42
N_MM_SLABS = 44
S_HK = 44
N_OUT_SLABS = 48


def _slab_table():
    t = [("raw", 0, None)] * N_MM_SLABS
    for h in range(NH):
        t[S_MQ + h] = ("normrope", 0, h)
        t[S_MK + h] = ("normrope", 1, None)
        t[S_NQ + h] = ("normrope", 2, NH + h)
        t[S_HF + h] = ("logf", h, None)
        t[S_MEMQ + h] = ("norm", 6, 2 * NH + h)
    t[S_KCMP] = ("normrope", 3, None)
    t[S_KSEL] = ("normrope", 4, None)
    t[S_KWIN] = ("normrope", 5, None)
    t[S_NGATE] = ("sigmoid", 0, None)
    t[S_NGATE + 1] = ("sigmoid", 0, None)
    return tuple(t)


_SLABS = _slab_table()


def _cparams(sem, vmem_mb=48):
    return pltpu.CompilerParams(dimension_semantics=sem, vmem_limit_bytes=vmem_mb * 1024 * 1024)


def _nt(a, b, precision=None):
    return lax.dot_general(a, b, (((1,), (1,)), ((), ())), preferred_element_type=f32, precision=precision)


def _tn(a, b, precision=None):
    return lax.dot_general(a, b, (((0,), (0,)), ((), ())), preferred_element_type=f32, precision=precision)


def _dot(a, b, precision=None):
    return jnp.dot(a, b, preferred_element_type=f32, precision=precision)


def _rms_cols(x, gain_col):
    return x * lax.rsqrt(jnp.mean(x * x, axis=0, keepdims=True) + EPS) * gain_col


def _softmax_rows(s, mask):
    s = jnp.where(mask, s, NEG)
    m = jnp.max(s, axis=-1, keepdims=True)
    e = jnp.where(mask, jnp.exp(s - m), 0.0)
    l = jnp.sum(e, axis=-1, keepdims=True)
    return e / jnp.where(l > 0.0, l, 1.0)


def _inproj_body(x_ref, g_ref, w_ref, cos_ref, sin_ref, gain_ref, lb_ref, p_ref, q_ref):
    xn = _rms_cols(x_ref[...], g_ref[...]).astype(bf16)
    cos = cos_ref[...]
    sin = sin_ref[...]
    for grp in range(N_MM_SLABS // 4):
        acc = _dot(w_ref[grp * 256:(grp + 1) * 256, :], xn)
        for s4 in range(4):
            slab = grp * 4 + s4
            kind, gi, qi = _SLABS[slab]
            y = acc[s4 * HD:(s4 + 1) * HD, :]
            r0 = slab * HD
            if kind in ("normrope", "norm"):
                y = _rms_cols(y, gain_ref[gi])
                if kind == "normrope":
                    y1 = y[0:ROPE_HALF]
                    y2 = y[ROPE_HALF:2 * ROPE_HALF]
                    y = jnp.concatenate([y1 * cos - y2 * sin, y2 * cos + y1 * sin, y[2 * ROPE_HALF:]], axis=0)
                p_ref[r0:r0 + HD, :] = y
                if qi is not None:
                    q_ref[qi] = y.T
            elif kind == "logf":
                rows = slice(gi * HD, (gi + 1) * HD)
                log_lb = lb_ref[0, rows, :]
                log1m_lb = lb_ref[1, rows, :]
                one_m_lb = lb_ref[2, rows, :]
                sp = jnp.log(1.0 + jnp.exp(-jnp.abs(y)))
                ls_pos = jnp.minimum(y, 0.0) - sp
                ls_neg = jnp.minimum(-y, 0.0) - sp
                b = log1m_lb + ls_pos
                mx = jnp.maximum(log_lb, b)
                logf = mx + jnp.log(1.0 + jnp.exp(-jnp.abs(log_lb - b)))
                p_ref[r0:r0 + HD, :] = logf
                p_ref[(S_HK + gi) * HD:(S_HK + gi + 1) * HD, :] = one_m_lb * jnp.exp(ls_neg)
            elif kind == "sigmoid":
                p_ref[r0:r0 + HD, :] = jax.nn.sigmoid(y)
            else:
                p_ref[r0:r0 + HD, :] = y


def _inproj(xT, g_col, wT, cosT, sinT, gains, lbp):
    n = xT.shape[1]
    full = lambda shape: pl.BlockSpec(shape, lambda t: (0,) * len(shape))
    return pl.pallas_call(
        _inproj_body,
        grid=(n // TQ,),
        in_specs=[
            pl.BlockSpec((D, TQ), lambda t: (0, t)),
            full((D, 1)),
            full((N_MM_SLABS * HD, D)),
            pl.BlockSpec((ROPE_HALF, TQ), lambda t: (0, t)),
            pl.BlockSpec((ROPE_HALF, TQ), lambda t: (0, t)),
            full((8, HD, 1)),
            full((3, MIXW, 1)),
        ],
        out_specs=[
            pl.BlockSpec((N_OUT_SLABS * HD, TQ), lambda t: (0, t)),
            pl.BlockSpec((3 * NH, TQ, HD), lambda t: (0, t, 0)),
        ],
        out_shape=[jax.ShapeDtypeStruct((N_OUT_SLABS * HD, n), f32),
                   jax.ShapeDtypeStruct((3 * NH, n, HD), f32)],
        compiler_params=_cparams(("parallel",)),
        name="inproj",
    )(xT, g_col, wT, cosT, sinT, gains, lbp)


def _memkv_body(x_ref, g_ref, w_ref, gain_ref, o_ref):
    xn = _rms_cols(x_ref[...], g_ref[...]).astype(bf16)
    for grp in range(2):
        acc = _dot(w_ref[grp * 256:(grp + 1) * 256, :], xn)
        for s4 in range(4):
            y = acc[s4 * HD:(s4 + 1) * HD, :]
            if grp == 0:
                y = _rms_cols(y, gain_ref[...])
            r0 = (grp * 4 + s4) * HD
            o_ref[r0:r0 + HD, :] = y


def _memkv(memT, g_col, wT, gain_col):
    n = memT.shape[1]
    full = lambda shape: pl.BlockSpec(shape, lambda t: (0,) * len(shape))
    return pl.pallas_call(
        _memkv_body,
        grid=(n // TQ,),
        in_specs=[pl.BlockSpec((D, TQ), lambda t: (0, t)), full((D, 1)), full((2 * MIXW, D)), full((HD, 1))],
        out_specs=pl.BlockSpec((2 * MIXW, TQ), lambda t: (0, t)),
        out_shape=jax.ShapeDtypeStruct((2 * MIXW, n), f32),
        compiler_params=_cparams(("parallel",)),
        name="memkv",
    )(memT, g_col, wT, gain_col)


def _moba_prompt_body(q_ref, kT_ref, vT_ref, o_ref, *, seq):
    c = pl.program_id(2)
    nblk = seq // MOBA_BLOCK
    q = q_ref[...]
    kT = kT_ref[...]
    jrow = lax.broadcasted_iota(jnp.int32, (seq, LANES), 0)
    ncol = lax.broadcasted_iota(jnp.int32, (seq, LANES), 1)
    pool = jnp.where(jrow // MOBA_BLOCK == ncol, 1.0 / MOBA_BLOCK, 0.0)
    kmeanT = _dot(kT, pool, HI)
    gate = _dot(q, kmeanT, HI)
    blk = lax.broadcasted_iota(jnp.int32, (1, LANES), 1)
    past = blk < c
    gm = jnp.where(past, gate, NEG)
    cnt = jnp.zeros(gm.shape, f32)
    for m in range(nblk):
        col = gm[:, m:m + 1]
        beats = (col > gm) | ((col == gm) & (m < blk))
        cnt = cnt + jnp.where(beats, 1.0, 0.0)
    sel = (past & (cnt < float(MOBA_TOPK))) | (blk == c)
    nrow = lax.broadcasted_iota(jnp.int32, (LANES, seq), 0)
    jcol = lax.broadcasted_iota(jnp.int32, (LANES, seq), 1)
    expand = jnp.where(jcol // MOBA_BLOCK == nrow, 1.0, 0.0).astype(bf16)
    selx = _dot(jnp.where(sel, 1.0, 0.0).astype(bf16), expand)
    pos = c * TQ + lax.broadcasted_iota(jnp.int32, (TQ, 1), 0)
    kpos = lax.broadcasted_iota(jnp.int32, (1, seq), 1)
    mask = (selx > 0.5) & (kpos <= pos)
    s = _dot(q.astype(bf16), kT.astype(bf16)) * SCALE
    p = _softmax_rows(s, mask)
    o_ref[...] = _nt(p.astype(bf16), vT_ref[...].astype(bf16))


def _moba_prompt(q_tok, p_all, batch, seq):
    nc = seq // TQ
    return pl.pallas_call(
        functools.partial(_moba_prompt_body, seq=seq),
        grid=(batch, NH, nc),
        in_specs=[
            pl.BlockSpec((None, TQ, HD), lambda b, h, c: (h, b * nc + c, 0)),
            pl.BlockSpec((HD, seq), lambda b, h, c: (S_MK + h, b)),
            pl.BlockSpec((HD, seq), lambda b, h, c: (S_MV + h, b)),
        ],
        out_specs=pl.BlockSpec((None, TQ, HD), lambda b, h, c: (h, b * nc + c, 0)),
        out_shape=jax.ShapeDtypeStruct((NH, batch * seq, HD), f32),
        compiler_params=_cparams(("parallel", "parallel", "parallel")),
        name="moba_prompt",
    )(q_tok, p_all, p_all)


def _compress_rows(x2_bf16, w_ref, pe_ref, w2k_ref, w2v_ref):
    nhb = x2_bf16.shape[0]
    w = w_ref[...]
    uv = _dot(x2_bf16, w)
    pe = _dot(pe_ref[...].astype(bf16), w)
    outs = []
    for i, w2_ref in enumerate((w2k_ref, w2v_ref)):
        lo = uv[:, (2 * i) * CMP_HID:(2 * i + 1) * CMP_HID]
        hi = uv[:, (2 * i + 1) * CMP_HID:(2 * i + 2) * CMP_HID]
        bias = pe[0:1, (2 * i) * CMP_HID:(2 * i + 1) * CMP_HID] + pe[8:9, (2 * i + 1) * CMP_HID:(2 * i + 2) * CMP_HID]
        pre = lo + pltpu.roll(hi, nhb - 1, 0) + bias
        hid = jax.nn.gelu(pre)
        outs.append(_dot(hid.astype(bf16), w2_ref[...].astype(bf16)))
    return outs


def _compress_body(x_ref, w_ref, pe_ref, w2k_ref, w2v_ref, kc_ref, vc_ref):
    kc, vc = _compress_rows(x_ref[...].astype(bf16), w_ref, pe_ref, w2k_ref, w2v_ref)
    kc_ref[...] = kc
    vc_ref[...] = vc


def _compress(x2, wbig, pe2, w2k, w2v):
    batch, nhb, _ = x2.shape
    full = lambda shape: pl.BlockSpec(shape, lambda b: (0,) * len(shape))
    return pl.pallas_call(
        _compress_body,
        grid=(batch,),
        in_specs=[pl.BlockSpec((None, nhb, 2 * CMP_STRIDE * HD), lambda b: (b, 0, 0)),
                  full((2 * CMP_STRIDE * HD, 4 * CMP_HID)), full((16, 2 * CMP_STRIDE * HD)),
                  full((CMP_HID, HD)), full((CMP_HID, HD))],
        out_specs=[pl.BlockSpec((None, nhb, HD), lambda b: (b, 0, 0))] * 2,
        out_shape=[jax.ShapeDtypeStruct((batch, nhb, HD), f32)] * 2,
        compiler_params=_cparams(("parallel",)),
        name="nsa_compress",
    )(x2, wbig, pe2, w2k, w2v)


def _nsa_prompt_body(q_ref, kc_ref, vc_ref, ks_ref, vs_ref, kw0, kw1, kw2, vw0, vw1, vw2, g_ref, o_ref, *, seq):
    c = pl.program_id(1)
    nhb = seq // CMP_STRIDE
    n_cmp = (seq - CMP_LEN) // CMP_STRIDE + 1
    n_sblk = seq // SEL_LEN
    n_top = min(NSA_TOPN, n_sblk)
    pos = c * TQ + lax.broadcasted_iota(jnp.int32, (TQ, 1), 0)
    pos_row = c * TQ + lax.broadcasted_iota(jnp.int32, (1, TQ), 1)

    ncol = lax.broadcasted_iota(jnp.int32, (1, nhb), 1)
    cmask = (ncol * CMP_STRIDE + (CMP_LEN - 1) <= pos) & (ncol < n_cmp)
    kc = kc_ref[...].astype(bf16)
    vc = vc_ref[...].astype(bf16)
    qs = [q_ref[h].astype(bf16) for h in range(NH)]
    psum = jnp.zeros((TQ, nhb), f32)
    o_cmp = []
    for h in range(NH):
        p = _softmax_rows(_nt(qs[h], kc) * SCALE, cmask)
        psum = psum + p
        o_cmp.append(_dot(p.astype(bf16), vc))

    srow = lax.broadcasted_iota(jnp.int32, (n_sblk, nhb), 0)
    nidx = lax.broadcasted_iota(jnp.int32, (n_sblk, nhb), 1)
    ovT = jnp.where((nidx * CMP_STRIDE <= srow * SEL_LEN + (SEL_LEN - 1))
                    & (nidx * CMP_STRIDE + (CMP_LEN - 1) >= srow * SEL_LEN) & (nidx < n_cmp), 1.0, 0.0)
    impT = _nt(ovT, psum, HI)
    sid = lax.broadcasted_iota(jnp.int32, (n_sblk, TQ), 0)
    cur = pos_row // SEL_LEN
    forced = (sid == 0) | (sid == cur) | (sid == cur - 1)
    score = jnp.where(sid > cur, NEG, impT + jnp.where(forced, NSA_FORCE, 0.0))
    cnt = jnp.zeros(score.shape, f32)
    for m in range(n_sblk):
        rm = score[m:m + 1, :]
        beats = (rm > score) | ((rm == score) & (m < sid))
        cnt = cnt + jnp.where(beats, 1.0, 0.0)
    selT = jnp.where((cnt < float(n_top)) & (sid <= cur), 1.0, 0.0)
    srow2 = lax.broadcasted_iota(jnp.int32, (n_sblk, seq), 0)
    jcol = lax.broadcasted_iota(jnp.int32, (n_sblk, seq), 1)
    expand = jnp.where(jcol // SEL_LEN == srow2, 1.0, 0.0).astype(bf16)
    selx = _tn(selT.astype(bf16), expand)
    kpos = lax.broadcasted_iota(jnp.int32, (1, seq), 1)
    smask = (selx > 0.5) & (kpos <= pos)

    ks = ks_ref[...].astype(bf16)
    vs = vs_ref[...].astype(bf16)
    kw = jnp.concatenate([kw2[...], kw1[...], kw0[...]], axis=1).astype(bf16)
    vw = jnp.concatenate([vw2[...], vw1[...], vw0[...]], axis=1).astype(bf16)
    wpos = (c - 2) * TQ + lax.broadcasted_iota(jnp.int32, (1, 3 * TQ), 1)
    dist = pos - wpos
    wmask = (wpos >= 0) & (dist >= 0) & (dist <= NSA_WINDOW)
    g = g_ref[...]
    for h in range(NH):
        p2 = _softmax_rows(_dot(qs[h], ks) * SCALE, smask)
        o_sel = _nt(p2.astype(bf16), vs)
        p3 = _softmax_rows(_dot(qs[h], kw) * SCALE, wmask)
        o_win = _nt(p3.astype(bf16), vw)
        o_ref[h] = (g[:, 3 * h:3 * h + 1] * o_cmp[h] + g[:, 3 * h + 1:3 * h + 2] * o_sel
                    + g[:, 3 * h + 2:3 * h + 3] * o_win)


def _nsa_prompt(q_tok, p_all, kc, vc, g_tok, batch, seq):
    nc = seq // TQ
    nhb = seq // CMP_STRIDE
    win = lambda slab, d: pl.BlockSpec((HD, TQ), lambda b, c: (slab, b * nc + jnp.maximum(c - d, 0)))
    return pl.pallas_call(
        functools.partial(_nsa_prompt_body, seq=seq),
        grid=(batch, nc),
        in_specs=[
            pl.BlockSpec((NH, TQ, HD), lambda b, c: (1, b * nc + c, 0)),
            pl.BlockSpec((None, nhb, HD), lambda b, c: (b, 0, 0)),
            pl.BlockSpec((None, nhb, HD), lambda b, c: (b, 0, 0)),
            pl.BlockSpec((HD, seq), lambda b, c: (S_KSEL, b)),
            pl.BlockSpec((HD, seq), lambda b, c: (S_VSEL, b)),
            win(S_KWIN, 0), win(S_KWIN, 1), win(S_KWIN, 2),
            win(S_VWIN, 0), win(S_VWIN, 1), win(S_VWIN, 2),
            pl.BlockSpec((TQ, LANES), lambda b, c: (b * nc + c, 0)),
        ],
        out_specs=pl.BlockSpec((NH, TQ, HD), lambda b, c: (0, b * nc + c, 0)),
        out_shape=jax.ShapeDtypeStruct((NH, batch * seq, HD), f32),
        compiler_params=_cparams(("parallel", "parallel")),
        name="nsa_prompt",
    )(q_tok, kc, vc, p_all, p_all, p_all, p_all, p_all, p_all, p_all, p_all, g_tok)


HG_CHUNK = 64


def _hgrn_body(q_ref, lf_ref, k_ref, v_ref, gt_ref, s0_ref, hn_ref, o_ref, sfin_ref, s_scr):
    j = pl.program_id(1)

    @pl.when(j == 0)
    def _():
        s_scr[...] = s0_ref[...]

    q = q_ref[...]
    g = lf_ref[...]
    k = k_ref[...]
    v = v_ref[...]
    lane = lax.broadcasted_iota(jnp.int32, (1, LANES), 1)
    in_a = lane < HG_CHUNK
    lane_c = lane % HG_CHUNK

    u = lax.broadcasted_iota(jnp.int32, (LANES, LANES), 0)
    t = lax.broadcasted_iota(jnp.int32, (LANES, LANES), 1)
    tri = jnp.where((u // HG_CHUNK == t // HG_CHUNK) & (u <= t), 1.0, 0.0)
    gc = _dot(g, tri, HI)
    gl_a = gc[:, HG_CHUNK - 1:HG_CHUNK]
    gl_b = gc[:, LANES - 1:LANES]
    qg = q * jnp.exp(gc)
    kd = k * jnp.exp(jnp.where(in_a, gl_a, gl_b) - gc)
    s_a = s_scr[...]
    o_a = _tn(s_a, jnp.where(in_a, qg, 0.0), HI)
    s_b = jnp.exp(gl_a) * s_a + _nt(jnp.where(in_a, kd, 0.0), v, HI)
    o_b = _tn(s_b, jnp.where(in_a, 0.0, qg), HI)
    s_next = jnp.exp(gl_b) * s_b + _nt(jnp.where(in_a, 0.0, kd), v, HI)
    s_scr[...] = s_next

    def band(d, carry):
        kr, vr, gr, dec, acc = carry
        a = jnp.sum(q * kr * jnp.exp(dec), axis=0, keepdims=True)
        a = jnp.where(lane_c >= d, a, 0.0)
        acc = acc + a * vr
        dec = dec + gr
        return (pltpu.roll(kr, 1, 1), pltpu.roll(vr, 1, 1), pltpu.roll(gr, 1, 1), dec, acc)

    zero = jnp.zeros((HD, LANES), f32)
    _, _, _, _, intra = lax.fori_loop(0, HG_CHUNK, band, (k, v, g, zero, zero))
    o = o_a + o_b + intra
    gt = gt_ref[...]
    o_ref[...] = _rms_cols(o, hn_ref[...]) * (gt * jax.nn.sigmoid(gt))

    @pl.when(j == pl.num_programs(1) - 1)
    def _():
        sfin_ref[...] = s_next


def _hgrn(src, s0, hn_col, n_seq, tiles, col_of):
    spec = lambda slab: pl.BlockSpec((HD, LANES), lambda s, j: (slab + s % NH, col_of(s // NH, j)))
    return pl.pallas_call(
        _hgrn_body,
        grid=(n_seq, tiles),
        in_specs=[spec(S_HQ), spec(S_HF), spec(S_HK), spec(S_HIN), spec(S_HGATE),
                  pl.BlockSpec((None, HD, HD), lambda s, j: (s, 0, 0)),
                  pl.BlockSpec((HD, 1), lambda s, j: (0, 0))],
        out_specs=[pl.BlockSpec((HD, LANES), lambda s, j: (s % NH, col_of(s // NH, j))),
                   pl.BlockSpec((None, HD, HD), lambda s, j: (s, 0, 0))],
        out_shape=[jax.ShapeDtypeStruct((MIXW, src.shape[1]), f32),
                   jax.ShapeDtypeStruct((n_seq, HD, HD), f32)],
        scratch_shapes=[pltpu.VMEM((HD, HD), f32)],
        compiler_params=_cparams(("parallel", "arbitrary")),
        name="hgrn",
    )(src, src, src, src, src, s0, hn_col)


def _mem_attn_body(q_ref, kT_ref, vT_ref, o_ref):
    s = _dot(q_ref[...].astype(bf16), kT_ref[...].astype(bf16)) * SCALE
    m = jnp.max(s, axis=-1, keepdims=True)
    e = jnp.exp(s - m)
    p = e / jnp.sum(e, axis=-1, keepdims=True)
    o_ref[...] = _nt(p.astype(bf16), vT_ref[...].astype(bf16))


def _mem_attn(q_tok, kv5, batch, tq, tiles, row0):
    return pl.pallas_call(
        _mem_attn_body,
        grid=(batch, NH, tiles),
        in_specs=[
            pl.BlockSpec((None, tq, HD), lambda b, h, c: (2 * NH + h, row0 // tq + b * tiles + c, 0)),
            pl.BlockSpec((None, None, None, HD, N_MEM), lambda b, h, c: (b, 0, h, 0, 0)),
            pl.BlockSpec((None, None, None, HD, N_MEM), lambda b, h, c: (b, 1, h, 0, 0)),
        ],
        out_specs=pl.BlockSpec((None, tq, HD), lambda b, h, c: (h, b * tiles + c, 0)),
        out_shape=jax.ShapeDtypeStruct((NH, batch * tiles * tq, HD), f32),
        compiler_params=_cparams(("parallel", "parallel", "parallel")),
        name="mem_attn",
    )(q_tok, kv5, kv5)


def _merge_body(x_ref, g_ref, moba_ref, nsa_ref, hg_ref, mem_ref, wg_ref, wb_ref, wbhg_ref, wo_ref, o_ref):
    x = x_ref[...]
    xn = _rms_cols(x, g_ref[...]).astype(bf16)
    y = jnp.zeros((D, TQ), f32)
    tok_refs = {0: moba_ref, 1: nsa_ref, 3: mem_ref}
    for n in range(4):
        gate = jax.nn.sigmoid(_dot(wg_ref[n * D:(n + 1) * D, :], xn))
        if n == 2:
            up = _dot(wbhg_ref[...], hg_ref[...].astype(bf16))
        else:
            up = jnp.zeros((D, TQ), f32)
            for h in range(NH):
                up = up + _nt(wb_ref[n, h], tok_refs[n][h].astype(bf16))
        y = y + gate * up
    o_ref[...] = x + _dot(wo_ref[...], y.astype(bf16))


def _merge(xT, g_col, br_moba, br_nsa, hgT, br_mem, wgT, wb, wbhgT, woT):
    n = xT.shape[1]
    full = lambda shape: pl.BlockSpec(shape, lambda t: (0,) * len(shape))
    tok = pl.BlockSpec((NH, TQ, HD), lambda t: (0, t, 0))
    return pl.pallas_call(
        _merge_body,
        grid=(n // TQ,),
        in_specs=[pl.BlockSpec((D, TQ), lambda t: (0, t)), full((D, 1)), tok, tok,
                  pl.BlockSpec((MIXW, TQ), lambda t: (0, t)), tok,
                  full((4 * D, D)), full((4, NH, D, HD)), full((D, MIXW)), full((D, D))],
        out_specs=pl.BlockSpec((D, TQ), lambda t: (0, t)),
        out_shape=jax.ShapeDtypeStruct((D, n), f32),
        compiler_params=_cparams(("parallel",), 56),
        name="merge",
    )(xT, g_col, br_moba, br_nsa, hgT, br_mem, wgT, wb, wbhgT, woT)


def _top_rows(x, k):
    rid = lax.broadcasted_iota(jnp.int32, x.shape, 0)
    vals = []
    for _ in range(k):
        m = jnp.max(x, axis=0, keepdims=True)
        idx = jnp.min(jnp.where(x == m, rid, x.shape[0]), axis=0, keepdims=True)
        x = jnp.where(rid == idx, -jnp.inf, x)
        vals.append(m)
    return vals


def _peer_scores_body(x_ref, g_ref, wq_ref, keys_ref, xn_ref, s1_ref, s2_ref, e1_ref, e2_ref, thr_ref):
    xn = _rms_cols(x_ref[...], g_ref[...]).astype(bf16)
    xn_ref[...] = xn
    k1 = keys_ref[0]
    k2 = keys_ref[1]
    thr_rows = []
    for h in range(PEER_HEADS):
        qh = _dot(wq_ref[h * 2 * PEER_KEYS:(h + 1) * 2 * PEER_KEYS, :], xn)
        s1 = _dot(k1, qh[0:PEER_KEYS], HI)
        s2 = _dot(k2, qh[PEER_KEYS:2 * PEER_KEYS], HI)
        v1 = _top_rows(s1, PEER_TOPK)
        v2 = _top_rows(s2, PEER_TOPK)
        v2m = jnp.concatenate(v2, axis=0)
        cand = jnp.concatenate([v1[a] + v2m for a in range(PEER_TOPK)], axis=0)
        sc = _top_rows(cand, PEER_TOPK)
        mx = v1[0] + v2[0]
        z = sc[0] * 0.0
        for r in range(PEER_TOPK):
            z = z + jnp.exp(sc[r] - mx)
        rows = slice(h * PEER_KEYS, (h + 1) * PEER_KEYS)
        s1_ref[rows, :] = s1
        s2_ref[rows, :] = s2
        e1_ref[rows, :] = jnp.exp(s1 - v1[0])
        e2_ref[rows, :] = jnp.exp(s2 - v2[0]) / z
        thr_rows.append(sc[PEER_TOPK - 1])
    thr_ref[...] = jnp.concatenate(thr_rows, axis=0)


def _peer_scores(xT, g_col, wqT, keys):
    n = xT.shape[1]
    full = lambda shape: pl.BlockSpec(shape, lambda t: (0,) * len(shape))
    tile = lambda rows: pl.BlockSpec((rows, TQ), lambda t: (0, t))
    hk = PEER_HEADS * PEER_KEYS
    return pl.pallas_call(
        _peer_scores_body,
        grid=(n // TQ,),
        in_specs=[tile(D), full((D, 1)), full((2 * hk, D)), full((2, PEER_KEYS, PEER_KEYS))],
        out_specs=[tile(D), tile(hk), tile(hk), tile(hk), tile(hk), tile(PEER_HEADS)],
        out_shape=[jax.ShapeDtypeStruct((D, n), bf16)] + [jax.ShapeDtypeStruct((hk, n), f32)] * 4
        + [jax.ShapeDtypeStruct((PEER_HEADS, n), f32)],
        compiler_params=_cparams(("parallel",)),
        name="peer_scores",
    )(xT, g_col, wqT, keys)


PEER_ETILE = 1024


def _peer_experts_body(xn_ref, s1_ref, s2_ref, e1_ref, e2_ref, thr_ref, wu_ref, wvT_ref, x_ref, o_ref, acc_ref):
    e = pl.program_id(1)

    @pl.when(e == 0)
    def _():
        acc_ref[...] = jnp.zeros_like(acc_ref)

    hT = _dot(wu_ref[...], xn_ref[...])
    parts = []
    for il in range(PEER_ETILE // PEER_KEYS):
        i = e * (PEER_ETILE // PEER_KEYS) + il
        w = jnp.zeros((PEER_KEYS, TQ), f32)
        for h in range(PEER_HEADS):
            s1r = s1_ref[pl.ds(h * PEER_KEYS + i, 1), :]
            e1r = e1_ref[pl.ds(h * PEER_KEYS + i, 1), :]
            rows = slice(h * PEER_KEYS, (h + 1) * PEER_KEYS)
            keep = s1r + s2_ref[rows, :] >= thr_ref[h:h + 1, :]
            w = w + jnp.where(keep, e1r * e2_ref[rows, :], 0.0)
        act = jax.nn.gelu(hT[il * PEER_KEYS:(il + 1) * PEER_KEYS, :])
        parts.append((w * act).astype(bf16))
    acc_ref[...] += _dot(wvT_ref[...], jnp.concatenate(parts, axis=0))

    @pl.when(e == pl.num_programs(1) - 1)
    def _():
        o_ref[...] = x_ref[...] + acc_ref[...]


def _peer_experts(xn, s1, s2, e1, e2, thr, wu, wvT, xT):
    n = xT.shape[1]
    hk = PEER_HEADS * PEER_KEYS
    tile = lambda rows: pl.BlockSpec((rows, TQ), lambda t, e: (0, t))
    return pl.pallas_call(
        _peer_experts_body,
        grid=(n // TQ, N_EXPERTS // PEER_ETILE),
        in_specs=[tile(D), tile(hk), tile(hk), tile(hk), tile(hk), tile(PEER_HEADS),
                  pl.BlockSpec((PEER_ETILE, D), lambda t, e: (e, 0)),
                  pl.BlockSpec((D, PEER_ETILE), lambda t, e: (0, e)),
                  tile(D)],
        out_specs=tile(D),
        out_shape=jax.ShapeDtypeStruct((D, n), f32),
        scratch_shapes=[pltpu.VMEM((D, TQ), f32)],
        compiler_params=_cparams(("parallel", "arbitrary")),
        name="peer_experts",
    )(xn, s1, s2, e1, e2, thr, wu, wvT, xT)


def _lane_pick(x, idx):
    lane = lax.broadcasted_iota(jnp.int32, (1, x.shape[1]), 1)
    return jnp.sum(jnp.where(lane == idx, x, 0.0), axis=1, keepdims=True)


def _moba_gate_body(pt_ref, q_ref, *refs, n_pages):
    pages = refs[:PG]
    sel_ref = refs[PG]
    km_ref = refs[PG + 1]
    j = pl.program_id(1)
    n_past = n_pages * PAGE // MOBA_BLOCK
    lane = lax.broadcasted_iota(jnp.int32, (1, LANES), 1)

    @pl.when(j == 0)
    def _():
        km_ref[...] = jnp.zeros_like(km_ref)

    acc = km_ref[...]
    for i in range(PG):
        blk = (j * PG + i) // (MOBA_BLOCK // PAGE)
        col = jnp.sum(pages[i][...], axis=1, keepdims=True)
        acc = acc + jnp.where(lane == blk, col, 0.0)
    km_ref[...] = acc

    @pl.when(j == pl.num_programs(1) - 1)
    def _():
        km = acc * (1.0 / MOBA_BLOCK)
        past = lane < n_past
        for h in range(NH):
            gate = _dot(q_ref[h], km[h * HD:(h + 1) * HD, :], HI)
            gm = jnp.where(past, gate, -jnp.inf)
            sel = jnp.where(lane == n_past, 1.0, 0.0) + jnp.zeros(gm.shape, f32)
            for _ in range(MOBA_TOPK):
                m = jnp.max(gm, axis=1, keepdims=True)
                idx = jnp.min(jnp.where(gm == m, lane, LANES), axis=1, keepdims=True)
                hit = (lane == idx) & (m > -jnp.inf)
                sel = jnp.where(hit, 1.0, sel)
                gm = jnp.where(hit, -jnp.inf, gm)
            sel_ref[h * q_ref.shape[1]:(h + 1) * q_ref.shape[1], :] = sel


def _moba_gate(page_table, q_tok, cache4, layer, row0, ts):
    bs, n_pages = page_table.shape
    page_spec = lambda i: pl.BlockSpec((None, None, MIXW, PAGE), lambda b, j, pt: (layer, pt[b, j * PG + i], 0, 0))
    return pl.pallas_call(
        functools.partial(_moba_gate_body, n_pages=n_pages),
        grid_spec=pltpu.PrefetchScalarGridSpec(
            num_scalar_prefetch=1,
            grid=(bs, n_pages // PG),
            in_specs=[pl.BlockSpec((NH, ts, HD), lambda b, j, pt: (0, row0 // ts + b, 0))]
            + [page_spec(i) for i in range(PG)],
            out_specs=pl.BlockSpec((None, NH * ts, LANES), lambda b, j, pt: (b, 0, 0)),
            scratch_shapes=[pltpu.VMEM((MIXW, LANES), f32)]),
        out_shape=jax.ShapeDtypeStruct((bs, NH * ts, LANES), f32),
        compiler_params=_cparams(("parallel", "arbitrary")),
        name="moba_sample_gate",
    )(page_table, q_tok, *([cache4] * PG))


def _flash_update(s, mask, vT, m_ref, l_ref, acc_ref, rows):
    s = jnp.where(mask, s, NEG)
    m_old = m_ref[rows, :]
    m_new = jnp.maximum(m_old, jnp.max(s, axis=-1, keepdims=True))
    alpha = jnp.exp(m_old - m_new)
    p = jnp.where(mask, jnp.exp(s - m_new), 0.0)
    l_ref[rows, :] = alpha * l_ref[rows, :] + jnp.sum(p, axis=-1, keepdims=True)
    acc_ref[rows, :] = alpha * acc_ref[rows, :] + _nt(p.astype(bf16), vT.astype(bf16))
    m_ref[rows, :] = m_new


def _own_mask(b, ts, shape):
    lane = lax.broadcasted_iota(jnp.int32, shape, 1)
    qi = lax.broadcasted_iota(jnp.int32, shape, 0) % ts
    return (lane // ts == b % (LANES // ts)) & (lane % ts <= qi)


def _moba_sample_body(pt_ref, q_ref, sel_ref, kn_ref, vn_ref, *refs, n_pages, ts):
    pages = refs[:PG]
    o_ref = refs[PG]
    m_ref, l_ref, acc_ref = refs[PG + 1:PG + 4]
    b = pl.program_id(0)
    j = pl.program_id(1)

    @pl.when(j == 0)
    def _():
        m_ref[...] = jnp.full(m_ref.shape, NEG, f32)
        l_ref[...] = jnp.zeros_like(l_ref)
        acc_ref[...] = jnp.zeros_like(acc_ref)

    sel = sel_ref[...]
    qs = [q_ref[h].astype(bf16) for h in range(NH)]
    for i in range(PG):
        blk = (j * PG + i) // (MOBA_BLOCK // PAGE)
        keep = _lane_pick(sel, blk) > 0.5
        for h in range(NH):
            rows = slice(h * ts, (h + 1) * ts)
            kT = pages[i][h * HD:(h + 1) * HD, :].astype(bf16)
            vT = pages[i][MIXW + h * HD:MIXW + (h + 1) * HD, :]
            s = _dot(qs[h], kT) * SCALE
            _flash_update(s, jnp.broadcast_to(keep[rows, :], s.shape), vT, m_ref, l_ref, acc_ref, rows)

    @pl.when(j == pl.num_programs(1) - 1)
    def _():
        own = _own_mask(b, ts, (ts, LANES))
        for h in range(NH):
            rows = slice(h * ts, (h + 1) * ts)
            s = _dot(qs[h], kn_ref[h * HD:(h + 1) * HD, :].astype(bf16)) * SCALE
            _flash_update(s, own, vn_ref[h * HD:(h + 1) * HD, :], m_ref, l_ref, acc_ref, rows)
        o_ref[...] = acc_ref[...] / l_ref[...]


def _moba_sample(page_table, q_tok, sel, p_all, cache4, layer, n_prompt, ts):
    bs, n_pages = page_table.shape
    page_spec = lambda i: pl.BlockSpec((None, None, 2 * MIXW, PAGE), lambda b, j, pt: (layer, pt[b, j * PG + i], 0, 0))
    new_spec = lambda slab0: pl.BlockSpec(
        (MIXW, LANES), lambda b, j, pt: (slab0 // NH, n_prompt // LANES + b // (LANES // ts)))
    rows = NH * ts
    return pl.pallas_call(
        functools.partial(_moba_sample_body, n_pages=n_pages, ts=ts),
        grid_spec=pltpu.PrefetchScalarGridSpec(
            num_scalar_prefetch=1,
            grid=(bs, n_pages // PG),
            in_specs=[pl.BlockSpec((NH, ts, HD), lambda b, j, pt: (0, n_prompt // ts + b, 0)),
                      pl.BlockSpec((None, rows, LANES), lambda b, j, pt: (b, 0, 0)),
                      new_spec(S_MK), new_spec(S_MV)]
            + [page_spec(i) for i in range(PG)],
            out_specs=pl.BlockSpec((None, rows, HD), lambda b, j, pt: (b, 0, 0)),
            scratch_shapes=[pltpu.VMEM((rows, 1), f32), pltpu.VMEM((rows, 1), f32), pltpu.VMEM((rows, HD), f32)]),
        out_shape=jax.ShapeDtypeStruct((bs, rows, HD), f32),
        compiler_params=_cparams(("parallel", "arbitrary")),
        name="moba_sample_attn",
    )(page_table, q_tok, sel, p_all, p_all, *([cache4] * PG))


def _nsa_select_body(pt_ref, q_ref, w_ref, pe_ref, w2k_ref, w2v_ref, *refs, n_pages, ts):
    pages = refs[:PG]
    oc_ref, sel_ref = refs[PG:PG + 2]
    x2_ref, tok_ref = refs[PG + 2:PG + 4]
    j = pl.program_id(1)
    past = n_pages * PAGE
    nhb = past // CMP_STRIDE
    n_cmp = (past + ts - CMP_LEN) // CMP_STRIDE + 1
    n_sblk = past // SEL_LEN + 1
    cur = past // SEL_LEN
    spad = sel_ref.shape[-1]
    hb_per_page = PAGE // CMP_STRIDE

    for i in range(PG):
        tok_ref[...] = pages[i][...].T
        r0 = pl.multiple_of((j * PG + i) * hb_per_page, hb_per_page)
        for r in range(CMP_STRIDE):
            x2_ref[pl.ds(r0, hb_per_page), r * LANES:(r + 1) * LANES] = tok_ref[pl.ds(r, hb_per_page, stride=CMP_STRIDE), :]

    @pl.when(j == pl.num_programs(1) - 1)
    def _():
        kc, vc = _compress_rows(x2_ref[...].astype(bf16), w_ref, pe_ref, w2k_ref, w2v_ref)
        q = jnp.concatenate([q_ref[h] for h in range(NH)], axis=0).astype(bf16)
        ncol = lax.broadcasted_iota(jnp.int32, (1, nhb), 1)
        p = _softmax_rows(_nt(q, kc.astype(bf16)) * SCALE, jnp.broadcast_to(ncol < n_cmp, (NH * ts, nhb)))
        oc_ref[...] = _dot(p.astype(bf16), vc.astype(bf16))
        psum = p[0:ts]
        for h in range(1, NH):
            psum = psum + p[h * ts:(h + 1) * ts]
        nrow = lax.broadcasted_iota(jnp.int32, (nhb, spad), 0)
        scol = lax.broadcasted_iota(jnp.int32, (nhb, spad), 1)
        ov = jnp.where((nrow * CMP_STRIDE <= scol * SEL_LEN + (SEL_LEN - 1))
                       & (nrow * CMP_STRIDE + (CMP_LEN - 1) >= scol * SEL_LEN) & (nrow < n_cmp), 1.0, 0.0)
        imp = _dot(psum, ov, HI)
        lane = lax.broadcasted_iota(jnp.int32, (1, spad), 1)
        forced = (lane == 0) | (lane == cur) | (lane == cur - 1)
        score = jnp.where(lane < n_sblk, imp + jnp.where(forced, NSA_FORCE, 0.0), -jnp.inf)
        sel = jnp.zeros(score.shape, f32)
        for _ in range(NSA_TOPN):
            m = jnp.max(score, axis=1, keepdims=True)
            idx = jnp.min(jnp.where(score == m, lane, spad), axis=1, keepdims=True)
            hit = (lane == idx) & (m > -jnp.inf)
            sel = jnp.where(hit, 1.0, sel)
            score = jnp.where(hit, -jnp.inf, score)
        sel_ref[...] = sel


def _nsa_select(page_table, q_tok, cache4, wbig, pe2, w2k, w2v, layer, n_prompt, ts, spad):
    bs, n_pages = page_table.shape
    nhb = n_pages * PAGE // CMP_STRIDE
    page_spec = lambda i: pl.BlockSpec((None, None, 2 * HD, PAGE), lambda b, j, pt: (layer, pt[b, j * PG + i], 0, 0))
    full = lambda shape: pl.BlockSpec(shape, lambda b, j, pt: (0,) * len(shape))
    return pl.pallas_call(
        functools.partial(_nsa_select_body, n_pages=n_pages, ts=ts),
        grid_spec=pltpu.PrefetchScalarGridSpec(
            num_scalar_prefetch=1,
            grid=(bs, n_pages // PG),
            in_specs=[pl.BlockSpec((NH, ts, HD), lambda b, j, pt: (1, n_prompt // ts + b, 0)),
                      full((2 * CMP_STRIDE * HD, 4 * CMP_HID)), full((16, 2 * CMP_STRIDE * HD)),
                      full((CMP_HID, HD)), full((CMP_HID, HD))]
            + [page_spec(i) for i in range(PG)],
            out_specs=[pl.BlockSpec((None, NH * ts, HD), lambda b, j, pt: (b, 0, 0)),
                       pl.BlockSpec((None, ts, spad), lambda b, j, pt: (b, 0, 0))],
            scratch_shapes=[pltpu.VMEM((nhb, 2 * CMP_STRIDE * HD), f32), pltpu.VMEM((PAGE, LANES), f32)]),
        out_shape=[jax.ShapeDtypeStruct((bs, NH * ts, HD), f32), jax.ShapeDtypeStruct((bs, ts, spad), f32)],
        compiler_params=_cparams(("parallel", "arbitrary"), 56),
        name="nsa_sample_select",
    )(page_table, q_tok, wbig, pe2, w2k, w2v, *([cache4] * PG))


def _nsa_sample_body(pt_ref, q_ref, sel_ref, oc_ref, g_ref, sn_ref, wn_ref, wbuf_ref, *refs, n_pages, ts):
    pages = refs[:PG]
    o_ref = refs[PG]
    m_ref, l_ref, acc_ref = refs[PG + 1:PG + 4]
    b = pl.program_id(0)
    j = pl.program_id(1)
    rows_all = slice(0, NH * ts)
    cur = n_pages * PAGE // SEL_LEN

    @pl.when(j == 0)
    def _():
        m_ref[...] = jnp.full(m_ref.shape, NEG, f32)
        l_ref[...] = jnp.zeros_like(l_ref)
        acc_ref[...] = jnp.zeros_like(acc_ref)

    sel = sel_ref[...]
    q = jnp.concatenate([q_ref[h] for h in range(NH)], axis=0).astype(bf16)
    lane = lax.broadcasted_iota(jnp.int32, (1, LANES), 1)
    for i in range(PG):
        blk0 = (j * PG + i) * (PAGE // SEL_LEN)
        keep8 = jnp.where(lane < SEL_LEN, _lane_pick(sel, blk0), _lane_pick(sel, blk0 + 1)) > 0.5
        keep = jnp.concatenate([keep8] * NH, axis=0)
        s = _dot(q, pages[i][0:HD, :].astype(bf16)) * SCALE
        _flash_update(s, keep, pages[i][HD:2 * HD, :], m_ref, l_ref, acc_ref, rows_all)

    @pl.when(j == pl.num_programs(1) - 1)
    def _():
        own = _own_mask(b, ts, (NH * ts, LANES))
        keep_own = jnp.concatenate([_lane_pick(sel, cur) > 0.5] * NH, axis=0)
        s = _dot(q, sn_ref[0:HD, :].astype(bf16)) * SCALE
        _flash_update(s, own & keep_own, sn_ref[HD:2 * HD, :], m_ref, l_ref, acc_ref, rows_all)
        o_sel = acc_ref[...] / l_ref[...]

        kw = jnp.concatenate([wbuf_ref[0], wn_ref[0:HD, :]], axis=1).astype(bf16)
        vw = jnp.concatenate([wbuf_ref[1], wn_ref[HD:2 * HD, :]], axis=1).astype(bf16)
        nbuf = wbuf_ref.shape[-1]
        qi = lax.broadcasted_iota(jnp.int32, (NH * ts, nbuf), 0) % ts
        bi = lax.broadcasted_iota(jnp.int32, (NH * ts, nbuf), 1)
        wmask = jnp.concatenate([bi >= qi, own], axis=1)
        p = _softmax_rows(_dot(q, kw) * SCALE, wmask)
        o_win = _nt(p.astype(bf16), vw)
        g = g_ref[...]
        oc = oc_ref[...]
        for h in range(NH):
            rows = slice(h * ts, (h + 1) * ts)
            o_ref[rows, :] = (g[:, 3 * h:3 * h + 1] * oc[rows, :] + g[:, 3 * h + 1:3 * h + 2] * o_sel[rows, :]
                              + g[:, 3 * h + 2:3 * h + 3] * o_win[rows, :])


def _nsa_sample(page_table, q_tok, sel, oc, g_tok, p_all, win5, cache4, layer, n_prompt, ts):
    bs, n_pages = page_table.shape
    spad = sel.shape[-1]
    nbuf = win5.shape[-1]
    rows = NH * ts
    page_spec = lambda i: pl.BlockSpec((None, None, 2 * HD, PAGE), lambda b, j, pt: (layer, pt[b, j * PG + i], 1, 0))
    new_spec = lambda slab0: pl.BlockSpec(
        (2 * HD, LANES), lambda b, j, pt: (slab0 // 2, n_prompt // LANES + b // (LANES // ts)))
    return pl.pallas_call(
        functools.partial(_nsa_sample_body, n_pages=n_pages, ts=ts),
        grid_spec=pltpu.PrefetchScalarGridSpec(
            num_scalar_prefetch=1,
            grid=(bs, n_pages // PG),
            in_specs=[pl.BlockSpec((NH, ts, HD), lambda b, j, pt: (1, n_prompt // ts + b, 0)),
                      pl.BlockSpec((None, ts, spad), lambda b, j, pt: (b, 0, 0)),
                      pl.BlockSpec((None, rows, HD), lambda b, j, pt: (b, 0, 0)),
                      pl.BlockSpec((ts, LANES), lambda b, j, pt: (n_prompt // ts + b, 0)),
                      new_spec(S_KSEL), new_spec(S_KWIN),
                      pl.BlockSpec((None, None, 2, HD, nbuf), lambda b, j, pt: (layer, b, 0, 0, 0))]
            + [page_spec(i) for i in range(PG)],
            out_specs=pl.BlockSpec((None, rows, HD), lambda b, j, pt: (b, 0, 0)),
            scratch_shapes=[pltpu.VMEM((rows, 1), f32), pltpu.VMEM((rows, 1), f32), pltpu.VMEM((rows, HD), f32)]),
        out_shape=jax.ShapeDtypeStruct((bs, rows, HD), f32),
        compiler_params=_cparams(("parallel", "arbitrary")),
        name="nsa_sample_attn",
    )(page_table, q_tok, sel, oc, g_tok, p_all, p_all, win5, *([cache4] * PG))


def _compress_weights(w1, pe):
    w1r = w1.reshape(2, 2, CMP_STRIDE, HD, CMP_HID)
    wbig = jnp.zeros((CMP_STRIDE, 2, HD, 4, CMP_HID), f32)
    for kv in range(2):
        for half in range(2):
            wbig = wbig.at[:, kv, :, 2 * kv + half, :].set(w1r[kv, half])
    wbig = wbig.reshape(2 * CMP_STRIDE * HD, 4 * CMP_HID).astype(bf16)
    per = pe.reshape(2, 2, CMP_STRIDE, HD)
    pe_rows = per.transpose(1, 2, 0, 3).reshape(2, 1, 2 * CMP_STRIDE * HD)
    pe2 = jnp.broadcast_to(pe_rows, (2, 8, 2 * CMP_STRIDE * HD)).reshape(16, 2 * CMP_STRIDE * HD)
    return wbig, pe2


def kernel(x_prompt, x_sample, cache_moba_kv, cache_nsa_kv, state_nsa_win, state_hgrn, cache_mem_kv, page_table, mem_prompt, norm_attn, norm_ffn, norm_mem, w_in, w_gate, w_branch, w_out, qk_gain, nsa_cmp_w1, nsa_cmp_w2, nsa_cmp_pe, hgrn_lb, hgrn_norm, w_mem_kv, peer_wq, peer_keys, peer_u, peer_v):
    bp, tp, _ = x_prompt.shape
    bs, ts, _ = x_sample.shape
    depth = w_in.shape[0]
    n_pages = page_table.shape[1]
    past = n_pages * PAGE
    n_p = bp * tp
    n_s = bs * ts
    n = n_p + n_s
    assert tp % TQ == 0 and n % TQ == 0 and n_p % LANES == 0 and LANES % ts == 0 and n_s % LANES == 0
    assert past % MOBA_BLOCK == 0 and n_pages % PG == 0 and past >= NSA_WINDOW and ts <= CMP_STRIDE
    n_sblk_s = past // SEL_LEN + 1
    spad = -(-n_sblk_s // LANES) * LANES

    pos = jnp.concatenate([jnp.tile(jnp.arange(tp, dtype=jnp.int32), bp),
                           past + jnp.tile(jnp.arange(ts, dtype=jnp.int32), bs)])
    inv_freq = ROPE_THETA ** (-jnp.arange(ROPE_HALF, dtype=f32) * (2.0 / (2 * ROPE_HALF)))
    ang = pos.astype(f32)[None, :] * inv_freq[:, None]
    cos_t = jnp.cos(ang)
    sin_t = jnp.sin(ang)

    lb_cum = jnp.cumsum(jax.nn.softmax(hgrn_lb.astype(f32), axis=0), axis=0)
    lb_all = lb_cum - lb_cum[0:1]

    moba_c = cache_moba_kv.transpose(0, 1, 3, 4, 5, 2).reshape(depth, -1, 2 * MIXW, PAGE)
    nsa_c = cache_nsa_kv.transpose(0, 1, 3, 4, 2).reshape(depth, -1, 4 * HD, PAGE)
    win_c = state_nsa_win.transpose(0, 1, 3, 4, 2)
    mem_c = cache_mem_kv.transpose(0, 1, 3, 4, 5, 2)

    x_t = jnp.concatenate([x_prompt.reshape(n_p, D), x_sample.reshape(n_s, D)], axis=0).T
    mem_t = mem_prompt.reshape(bp * N_MEM, D).T

    outs = {k: [] for k in ("moba_p", "moba_s", "nsa_p", "nsa_s", "win_p", "win_s", "hg_p", "hg_s", "mem_p")}
    for l in range(depth):
        w = w_in[l]
        w_re = jnp.concatenate([w[:, 0:1408], w[:, 1420:2700], w[:, 1408:1420],
                                jnp.zeros((D, N_MM_SLABS * HD - 2700), f32)], axis=1)
        lb = lb_all[l][:, None]
        lbp = jnp.stack([jnp.log(lb), jnp.log1p(-lb), 1.0 - lb])
        p_all, q_tok = _inproj(x_t, norm_attn[l][:, None], w_re.T.astype(bf16), cos_t, sin_t,
                               qk_gain[l][:, :, None], lbp)

        def tok(slab0, nslab, lo, hi):
            return p_all[slab0 * HD:(slab0 + nslab) * HD, lo:hi].T

        for tag, lo, hi, bb, tt in (("p", 0, n_p, bp, tp), ("s", n_p, n, bs, ts)):
            kk = tok(S_MK, NH, lo, hi).reshape(bb, tt, NH, HD)
            vv = tok(S_MV, NH, lo, hi).reshape(bb, tt, NH, HD)
            outs["moba_" + tag].append(jnp.stack([kk, vv], axis=2))
            outs["nsa_" + tag].append(tok(S_KCMP, 4, lo, hi).reshape(bb, tt, 4, HD))
        win_new_p = tok(S_KWIN, 2, 0, n_p).reshape(bp, tp, 2, HD)
        outs["win_p"].append(win_new_p[:, tp - min(NSA_WINDOW, tp):])
        win_new_s = tok(S_KWIN, 2, n_p, n).reshape(bs, ts, 2, HD)
        win_ctx = jnp.concatenate([state_nsa_win[l], win_new_s], axis=1)
        outs["win_s"].append(win_ctx[:, win_ctx.shape[1] - min(NSA_WINDOW, win_ctx.shape[1]):])

        g_tok = p_all[S_NGATE * HD:(S_NGATE + 2) * HD, :].T

        moba_p = _moba_prompt(q_tok, p_all, bp, tp)
        sel_m = _moba_gate(page_table, q_tok, moba_c, l, n_p, ts)
        moba_s = _moba_sample(page_table, q_tok, sel_m, p_all, moba_c, l, n_p, ts)
        br_moba = jnp.concatenate([moba_p, moba_s.reshape(bs, NH, ts, HD).transpose(1, 0, 2, 3).reshape(NH, n_s, HD)],
                                  axis=1)

        wbig, pe2 = _compress_weights(nsa_cmp_w1[l], nsa_cmp_pe[l])
        w2k, w2v = nsa_cmp_w2[l, 0], nsa_cmp_w2[l, 1]
        x2 = (p_all[S_KCMP * HD:(S_KCMP + 2) * HD, :n_p].reshape(2, HD, bp, tp).transpose(2, 3, 0, 1)
              .reshape(bp, tp // CMP_STRIDE, 2 * CMP_STRIDE * HD))
        kc, vc = _compress(x2, wbig, pe2, w2k, w2v)
        nsa_p = _nsa_prompt(q_tok, p_all, kc, vc, g_tok, bp, tp)
        oc_s, sel_n = _nsa_select(page_table, q_tok, nsa_c, wbig, pe2, w2k, w2v, l, n_p, ts, spad)
        nsa_s = _nsa_sample(page_table, q_tok, sel_n, oc_s, g_tok, p_all, win_c, nsa_c, l, n_p, ts)
        br_nsa = jnp.concatenate([nsa_p, nsa_s.reshape(bs, NH, ts, HD).transpose(1, 0, 2, 3).reshape(NH, n_s, HD)],
                                 axis=1)

        hn = hgrn_norm[l][:, None]
        tiles_p = tp // LANES
        hg_p, s_p = _hgrn(p_all, jnp.zeros((bp * NH, HD, HD), f32), hn, bp * NH, tiles_p,
                          lambda b, j: b * tiles_p + j)
        p_smp = jnp.pad(p_all[:, n_p:].reshape(-1, bs, ts), ((0, 0), (0, 0), (0, LANES - ts))).reshape(-1, bs * LANES)
        hg_s, s_s = _hgrn(p_smp, state_hgrn[l].reshape(bs * NH, HD, HD), hn, bs * NH, 1, lambda b, j: b)
        hg_t = jnp.concatenate([hg_p[:, :n_p], hg_s.reshape(MIXW, bs, LANES)[:, :, :ts].reshape(MIXW, n_s)], axis=1)
        outs["hg_p"].append(s_p.reshape(bp, NH, HD, HD))
        outs["hg_s"].append(s_s.reshape(bs, NH, HD, HD))

        mkv = _memkv(mem_t, norm_mem[l][:, None], w_mem_kv[l].T.astype(bf16), qk_gain[l, 7][:, None])
        outs["mem_p"].append(mkv.T.reshape(bp, N_MEM, 2, NH, HD))
        mkv5 = mkv.reshape(2, NH, HD, bp, N_MEM).transpose(3, 0, 1, 2, 4)
        mem_p = _mem_attn(q_tok, mkv5, bp, TQ, tp // TQ, 0)
        mem_s = _mem_attn(q_tok, mem_c[l], bs, ts, 1, n_p)
        br_mem = jnp.concatenate([mem_p, mem_s], axis=1)

        wb = w_branch[l].reshape(4, NH, HD, D).transpose(0, 1, 3, 2).astype(bf16)
        x_t = _merge(x_t, norm_attn[l][:, None], br_moba, br_nsa, hg_t, br_mem,
                     w_gate[l].T.astype(bf16), wb, w_branch[l, 2].T.astype(bf16), w_out[l].T.astype(bf16))

        xn2, s1, s2, e1, e2, thr = _peer_scores(x_t, norm_ffn[l][:, None], peer_wq[l].T.astype(bf16), peer_keys[l])
        x_t = _peer_experts(xn2, s1, s2, e1, e2, thr, peer_u[l].astype(bf16), peer_v[l].T.astype(bf16), x_t)

    y = x_t.T
    return (y[:n_p].reshape(bp, tp, D), y[n_p:].reshape(bs, ts, D),
            jnp.stack(outs["moba_p"]), jnp.stack(outs["moba_s"]), jnp.stack(outs["nsa_p"]), jnp.stack(outs["nsa_s"]),
            jnp.stack(outs["win_p"]), jnp.stack(outs["win_s"]), jnp.stack(outs["hg_p"]), jnp.stack(outs["hg_s"]),
            jnp.stack(outs["mem_p"]))
```
